```python
import jax, jax.numpy as jnp
from jax import lax
import numpy as np

D_MODEL = 2048
BATCH = 4
SEQ = 2048
DEPTH = 1
DEC_BATCH = 128
DEC_SEQ = 4
PAST_LEN = 2048
PAGE_SIZE = 128

HEAD_DIM = 128
SB_HEADS = D_MODEL // (2 * HEAD_DIM)
SB_WIDTH = SB_HEADS * HEAD_DIM
SB_LOGIT_OFFSET = -6.0
MEM_HEADS = 4
MEM_WIDTH = MEM_HEADS * HEAD_DIM
MEM_LEN = 256
CONV_WIDTH = D_MODEL // 4
CONV_TAPS = 31
CONV_HIST = CONV_TAPS - 1
N_BRANCH = 3
D_FF = 4 * D_MODEL
Q_BLOCK = 128
EPS = 1e-6
IN_COLS = 2 * CONV_WIDTH + 3 * SB_WIDTH + MEM_WIDTH + N_BRANCH * D_MODEL
SPLIT_POINTS = (CONV_WIDTH, 2 * CONV_WIDTH, 2 * CONV_WIDTH + SB_WIDTH, 2 * CONV_WIDTH + 2 * SB_WIDTH,
                2 * CONV_WIDTH + 3 * SB_WIDTH, 2 * CONV_WIDTH + 3 * SB_WIDTH + MEM_WIDTH)

kernel_name = 'gated_conformer_stickbreak_memory_decoder_step'


def rms_norm(x, g):
    xf = x.astype(jnp.float32)
    y = xf * lax.rsqrt(jnp.mean(xf * xf, axis=-1, keepdims=True) + EPS)
    return (y * g.astype(jnp.float32)).astype(x.dtype)


def layer_norm(x, g, b):
    xf = x.astype(jnp.float32)
    xc = xf - jnp.mean(xf, axis=-1, keepdims=True)
    y = xc * lax.rsqrt(jnp.mean(xc * xc, axis=-1, keepdims=True) + EPS)
    return (y * g.astype(jnp.float32) + b.astype(jnp.float32)).astype(x.dtype)


def depthwise_causal_conv(u_ext, w, b):
    y = lax.conv_general_dilated(u_ext, w[:, None, :].astype(u_ext.dtype), window_strides=(1,), padding='VALID',
                                 dimension_numbers=('NWC', 'WIO', 'NWC'), feature_group_count=u_ext.shape[-1])
    return y + b


def stick_breaking_block(q, k, v, q_pos, k_pos, b_sb):
    z = jnp.einsum('bqhd,bkhd->bhqk', q, k).astype(jnp.float32) * (HEAD_DIM ** -0.5)
    z = z + b_sb.astype(jnp.float32)[None, :, None, None]
    causal = k_pos[None, :] < q_pos[:, None]
    log_keep = jnp.where(causal, jax.nn.log_sigmoid(-z), 0.0)
    after = lax.cumsum(log_keep, axis=3, reverse=True) - log_keep
    a = jnp.where(causal, jnp.exp(jax.nn.log_sigmoid(z) + after), 0.0)
    return jnp.einsum('bhqk,bkhd->bqhd', a.astype(v.dtype), v)


def stick_breaking_sweep(q, k_all, v_all, past_len, b_sb):
    t = q.shape[1]
    outs = []
    for qs in range(0, t, Q_BLOCK):
        qe = min(t, qs + Q_BLOCK)
        kend = past_len + qe
        q_pos = past_len + jnp.arange(qs, qe)
        k_pos = jnp.arange(kend)
        outs.append(stick_breaking_block(q[:, qs:qe], k_all[:, :kend], v_all[:, :kend], q_pos, k_pos, b_sb))
    return jnp.concatenate(outs, axis=1)


def memory_kv(mem, g_mem, w_mem_kv, g_k_mem):
    b, m, _ = mem.shape
    kv = (rms_norm(mem, g_mem) @ w_mem_kv).reshape(b, m, 2, MEM_HEADS, HEAD_DIM)
    return rms_norm(kv[:, :, 0], g_k_mem), kv[:, :, 1]


def memory_attend(q, k, v):
    s = jnp.einsum('bqhd,bmhd->bhqm', q, k).astype(jnp.float32) * (HEAD_DIM ** -0.5)
    p = jax.nn.softmax(s, axis=-1)
    return jnp.einsum('bhqm,bmhd->bqhd', p.astype(v.dtype), v)


def trunk_layer(x, conv_hist, past_k, past_v, mem_k, mem_v, p):
    b, t, _ = x.shape
    past_len = past_k.shape[1]
    h = rms_norm(x, p['g_mix'])
    proj = h @ p['w_in']
    u_lin, u_gate, q_sb, k_sb, v_sb, q_mem, gate_logits = jnp.split(proj, SPLIT_POINTS, axis=-1)
    u = u_lin * jax.nn.sigmoid(u_gate)
    u_ext = jnp.concatenate([conv_hist, u], axis=1)
    c = depthwise_causal_conv(u_ext, p['w_dw'], p['b_dw'])
    c = jax.nn.silu(layer_norm(c, p['g_conv_ln'], p['b_conv_ln']))
    y_conv = c @ p['w_conv_out']
    q = rms_norm(q_sb.reshape(b, t, SB_HEADS, HEAD_DIM), p['g_q_sb'])
    k = rms_norm(k_sb.reshape(b, t, SB_HEADS, HEAD_DIM), p['g_k_sb'])
    v = v_sb.reshape(b, t, SB_HEADS, HEAD_DIM)
    k_all = jnp.concatenate([past_k, k], axis=1)
    v_all = jnp.concatenate([past_v, v], axis=1)
    o = stick_breaking_sweep(q, k_all, v_all, past_len, p['b_sb'])
    y_sb = o.reshape(b, t, SB_WIDTH) @ p['w_sb_out']
    qm = rms_norm(q_mem.reshape(b, t, MEM_HEADS, HEAD_DIM), p['g_q_mem'])
    y_mem = memory_attend(qm, mem_k, mem_v).reshape(b, t, MEM_WIDTH) @ p['w_mem_out']
    gates = jax.nn.sigmoid((gate_logits + p['b_gate']).astype(jnp.float32)).astype(x.dtype)
    gates = gates.reshape(b, t, N_BRANCH, D_MODEL)
    merged = gates[:, :, 0] * y_conv + gates[:, :, 1] * y_sb + gates[:, :, 2] * y_mem
    x = x + merged @ p['w_o']
    h2 = rms_norm(x, p['g_mlp'])
    x = x + jnp.square(jax.nn.relu(h2 @ p['w_up'])) @ p['w_down']
    return x, u_ext[:, -CONV_HIST:], k, v


def setup_inputs(seed: int = 0) -> dict:
    key = jax.random.key(seed)
    ks = jax.random.split(key, 40)

    def nrm(i, shape, scale=1.0):
        return scale * jax.random.normal(ks[i], shape, jnp.float32)

    def gain(i, shape):
        return 1.0 + nrm(i, shape, 0.02)

    n_pages = PAST_LEN // PAGE_SIZE
    n_used = DEC_BATCH * n_pages
    n_pool = n_used + (n_used + 3) // 4
    page_table = jax.random.permutation(ks[0], n_pool)[:n_used].reshape(DEC_BATCH, n_pages).astype(jnp.int32)
    return {
        'x_prompt': nrm(1, (BATCH, SEQ, D_MODEL)),
        'x_sample': nrm(2, (DEC_BATCH, DEC_SEQ, D_MODEL)),
        'mem_prompt': nrm(3, (BATCH, MEM_LEN, D_MODEL)),
        'cache_sb_k': nrm(4, (DEPTH, n_pool, PAGE_SIZE, SB_HEADS, HEAD_DIM)),
        'cache_sb_v': nrm(5, (DEPTH, n_pool, PAGE_SIZE, SB_HEADS, HEAD_DIM)),
        'page_table': page_table,
        'state_conv': nrm(6, (DEPTH, DEC_BATCH, CONV_HIST, CONV_WIDTH), 0.5),
        'cache_mem_k': nrm(7, (DEPTH, DEC_BATCH, MEM_LEN, MEM_HEADS, HEAD_DIM)),
        'cache_mem_v': nrm(8, (DEPTH, DEC_BATCH, MEM_LEN, MEM_HEADS, HEAD_DIM)),
        'g_mix': gain(9, (DEPTH, D_MODEL)),
        'w_in': nrm(10, (DEPTH, D_MODEL, IN_COLS), D_MODEL ** -0.5),
        'b_gate': nrm(11, (DEPTH, N_BRANCH * D_MODEL), 0.01),
        'w_dw': nrm(12, (DEPTH, CONV_TAPS, CONV_WIDTH), CONV_TAPS ** -0.5),
        'b_dw': nrm(13, (DEPTH, CONV_WIDTH), 0.01),
        'g_conv_ln': gain(14, (DEPTH, CONV_WIDTH)),
        'b_conv_ln': nrm(15, (DEPTH, CONV_WIDTH), 0.01),
        'w_conv_out': nrm(16, (DEPTH, CONV_WIDTH, D_MODEL), CONV_WIDTH ** -0.5),
        'g_q_sb': gain(17, (DEPTH, HEAD_DIM)),
        'g_k_sb': gain(18, (DEPTH, HEAD_DIM)),
        'b_sb': SB_LOGIT_OFFSET + nrm(29, (DEPTH, SB_HEADS), 0.1),
        'w_sb_out': nrm(19, (DEPTH, SB_WIDTH, D_MODEL), SB_WIDTH ** -0.5),
        'g_mem': gain(20, (DEPTH, D_MODEL)),
        'w_mem_kv': nrm(21, (DEPTH, D_MODEL, 2 * MEM_WIDTH), D_MODEL ** -0.5),
        'g_q_mem': gain(22, (DEPTH, HEAD_DIM)),
        'g_k_mem': gain(23, (DEPTH, HEAD_DIM)),
        'w_mem_out': nrm(24, (DEPTH, MEM_WIDTH, D_MODEL), MEM_WIDTH ** -0.5),
        'w_o': nrm(25, (DEPTH, D_MODEL, D_MODEL), D_MODEL ** -0.5),
        'g_mlp': gain(26, (DEPTH, D_MODEL)),
        'w_up': nrm(27, (DEPTH, D_MODEL, D_FF), D_MODEL ** -0.5),
        'w_down': nrm(28, (DEPTH, D_FF, D_MODEL), D_FF ** -0.5),
    }


def reference(x_prompt, x_sample, mem_prompt, cache_sb_k, cache_sb_v, page_table, state_conv, cache_mem_k,
              cache_mem_v, g_mix, w_in, b_gate, w_dw, b_dw, g_conv_ln, b_conv_ln, w_conv_out, g_q_sb, g_k_sb,
              b_sb, w_sb_out, g_mem, w_mem_kv, g_q_mem, g_k_mem, w_mem_out, w_o, g_mlp, w_up, w_down):
    n_pages = PAST_LEN // PAGE_SIZE
    past_len = n_pages * PAGE_SIZE
    bp = x_prompt.shape[0]
    bs = x_sample.shape[0]
    zero_hist = jnp.zeros((bp, CONV_HIST, CONV_WIDTH), x_prompt.dtype)
    no_past = jnp.zeros((bp, 0, SB_HEADS, HEAD_DIM), x_prompt.dtype)
    yp, ys = x_prompt, x_sample
    kp_l, vp_l, ks_l, vs_l, cp_l, cs_l, mk_l, mv_l = [], [], [], [], [], [], [], []
    for l in range(DEPTH):
        p = {'g_mix': g_mix[l], 'w_in': w_in[l], 'b_gate': b_gate[l], 'w_dw': w_dw[l], 'b_dw': b_dw[l],
             'g_conv_ln': g_conv_ln[l], 'b_conv_ln': b_conv_ln[l], 'w_conv_out': w_conv_out[l],
             'g_q_sb': g_q_sb[l], 'g_k_sb': g_k_sb[l], 'b_sb': b_sb[l], 'w_sb_out': w_sb_out[l],
             'g_q_mem': g_q_mem[l], 'w_mem_out': w_mem_out[l], 'w_o': w_o[l], 'g_mlp': g_mlp[l],
             'w_up': w_up[l], 'w_down': w_down[l]}
        mk, mv = memory_kv(mem_prompt, g_mem[l], w_mem_kv[l], g_k_mem[l])
        yp, cp, kp, vp = trunk_layer(yp, zero_hist, no_past, no_past, mk, mv, p)
        past_k = cache_sb_k[l][page_table].reshape(bs, past_len, SB_HEADS, HEAD_DIM)
        past_v = cache_sb_v[l][page_table].reshape(bs, past_len, SB_HEADS, HEAD_DIM)
        ys, cs, ks_new, vs_new = trunk_layer(ys, state_conv[l], past_k, past_v, cache_mem_k[l], cache_mem_v[l], p)
        kp_l.append(kp); vp_l.append(vp); ks_l.append(ks_new); vs_l.append(vs_new)
        cp_l.append(cp); cs_l.append(cs); mk_l.append(mk); mv_l.append(mv)
    return (yp, ys, jnp.stack(kp_l), jnp.stack(vp_l), jnp.stack(ks_l), jnp.stack(vs_l),
            jnp.stack(cp_l), jnp.stack(cs_l), jnp.stack(mk_l), jnp.stack(mv_l))
```

```python
import functools

import jax
import jax.numpy as jnp
from jax import lax
from jax.experimental import pallas as pl
from jax.experimental.pallas import tpu as pltpu

EPS = 1e-6
HEAD_DIM = 128
SB_HEADS = 8
MEM_HEADS = 4
MEM_LEN = 256
CONV_TAPS = 31
CONV_HIST = CONV_TAPS - 1
CONV_HALO = 32
PAGE_SIZE = 128
PAGE_ROWS = PAGE_SIZE * SB_HEADS
LANES = 128
COL_BLOCK = 512
SCALE = HEAD_DIM ** -0.5
MASKED = -1e30
VMEM_LIMIT_BYTES = 56 * 1024 * 1024

_NT = (((1,), (1,)), ((), ()))


def _params(*semantics):
    return pltpu.CompilerParams(dimension_semantics=semantics, vmem_limit_bytes=VMEM_LIMIT_BYTES)


def _rms_rows_to_bf16(x_ref, g_ref, h_ref, rows):
    def body(r, carry):
        sl = pl.ds(pl.multiple_of(r * 128, 128), 128)
        x = x_ref[sl, :]
        ms = jnp.mean(x * x, axis=-1, keepdims=True)
        h_ref[sl, :] = (x * lax.rsqrt(ms + EPS) * g_ref[...]).astype(jnp.bfloat16)
        return carry
    lax.fori_loop(0, rows // 128, body, 0)


def _softplus_neg_abs(z):
    return jnp.log1p(jnp.exp(-jnp.abs(z)))


def _split_bf16(x):
    hi = x.astype(jnp.bfloat16)
    lo = (x - hi.astype(jnp.float32)).astype(jnp.bfloat16)
    return hi, lo


def _norm_proj_kernel(x_ref, g_ref, w_ref, gcol_ref, bcol_ref, o_ref, h_ref, *, tm, tn, norm_blocks, gate_start):
    j = pl.program_id(1)

    @pl.when(j == 0)
    def _():
        _rms_rows_to_bf16(x_ref, g_ref, h_ref, tm)

    acc = jnp.dot(h_ref[...], w_ref[...], preferred_element_type=jnp.float32)

    is_norm = functools.reduce(jnp.logical_or, [j == b for b in norm_blocks], j < 0)
    is_gate = j >= gate_start

    @pl.when(is_norm)
    def _():
        for c in range(tn // HEAD_DIM):
            sl = slice(c * HEAD_DIM, (c + 1) * HEAD_DIM)
            blk = acc[:, sl]
            ms = jnp.mean(blk * blk, axis=-1, keepdims=True)
            o_ref[:, sl] = blk * lax.rsqrt(ms + EPS) * gcol_ref[:, sl]

    @pl.when(is_gate)
    def _():
        o_ref[...] = jax.nn.sigmoid(acc + bcol_ref[...])

    @pl.when(jnp.logical_not(jnp.logical_or(is_norm, is_gate)))
    def _():
        o_ref[...] = acc


def _norm_proj(x, g, w_bf16, gcol, bcol, *, tm, tn, norm_blocks, gate_start):
    m, k = x.shape
    n = w_bf16.shape[1]
    kern = functools.partial(_norm_proj_kernel, tm=tm, tn=tn, norm_blocks=tuple(norm_blocks), gate_start=gate_start)
    return pl.pallas_call(
        kern,
        grid=(m // tm, n // tn),
        in_specs=[
            pl.BlockSpec((tm, k), lambda i, j: (i, 0)),
            pl.BlockSpec((1, k), lambda i, j: (0, 0)),
            pl.BlockSpec((k, tn), lambda i, j: (0, j)),
            pl.BlockSpec((1, tn), lambda i, j: (0, j)),
            pl.BlockSpec((1, tn), lambda i, j: (0, j)),
        ],
        out_specs=pl.BlockSpec((tm, tn), lambda i, j: (i, j)),
        out_shape=jax.ShapeDtypeStruct((m, n), jnp.float32),
        scratch_shapes=[pltpu.VMEM((tm, k), jnp.bfloat16)],
        compiler_params=_params("arbitrary", "arbitrary"),
        name="norm_proj",
    )(x, g.reshape(1, k), w_bf16, gcol, bcol)


def _conv_kernel(ul_ref, ug_ref, hist_ref, wdw_ref, bdw_ref, gln_ref, bln_ref, c_ref, st_ref, ext_ref, *, tm):
    t = pl.program_id(1)
    nt = pl.num_programs(1)
    off = CONV_HALO - CONV_HIST

    @pl.when(t == 0)
    def _():
        ext_ref[0:off, :] = jnp.zeros((off, ext_ref.shape[1]), jnp.float32)
        ext_ref[off:CONV_HALO, :] = hist_ref[0]

    @pl.when(t > 0)
    def _():
        ext_ref[0:CONV_HALO, :] = ext_ref[tm:tm + CONV_HALO, :]

    u = ul_ref[0] * jax.nn.sigmoid(ug_ref[0])
    ext_ref[CONV_HALO:CONV_HALO + tm, :] = u

    acc = wdw_ref[0:1, :] * ext_ref[off:off + tm, :]
    for j in range(1, CONV_TAPS):
        acc = acc + wdw_ref[j:j + 1, :] * ext_ref[off + j:off + j + tm, :]
    c = acc + bdw_ref[...]
    cc = c - jnp.mean(c, axis=-1, keepdims=True)
    y = cc * lax.rsqrt(jnp.mean(cc * cc, axis=-1, keepdims=True) + EPS)
    y = y * gln_ref[...] + bln_ref[...]
    c_ref[0] = (y * jax.nn.sigmoid(y)).astype(jnp.bfloat16)

    @pl.when(t == nt - 1)
    def _():
        st_ref[0] = ext_ref[tm + off:tm + CONV_HALO, :]


def _conv_branch(proj3, hist, w_dw, b_dw, g_ln, b_ln, *, tm):
    b, s, _ = proj3.shape
    cw = w_dw.shape[1]
    kern = functools.partial(_conv_kernel, tm=tm)
    row = lambda v: v.reshape(1, cw)
    return pl.pallas_call(
        kern,
        grid=(b, s // tm),
        in_specs=[
            pl.BlockSpec((1, tm, cw), lambda i, t: (i, t, 0)),
            pl.BlockSpec((1, tm, cw), lambda i, t: (i, t, 1)),
            pl.BlockSpec((1, CONV_HIST, cw), lambda i, t: (i, 0, 0)),
            pl.BlockSpec((CONV_TAPS, cw), lambda i, t: (0, 0)),
            pl.BlockSpec((1, cw), lambda i, t: (0, 0)),
            pl.BlockSpec((1, cw), lambda i, t: (0, 0)),
            pl.BlockSpec((1, cw), lambda i, t: (0, 0)),
        ],
        out_specs=[
            pl.BlockSpec((1, tm, cw), lambda i, t: (i, t, 0)),
            pl.BlockSpec((1, CONV_HIST, cw), lambda i, t: (i, 0, 0)),
        ],
        out_shape=[
            jax.ShapeDtypeStruct((b, s, cw), jnp.bfloat16),
            jax.ShapeDtypeStruct((b, CONV_HIST, cw), jnp.float32),
        ],
        scratch_shapes=[pltpu.VMEM((CONV_HALO + tm, cw), jnp.float32)],
        compiler_params=_params("arbitrary", "arbitrary"),
        name="conv_branch",
    )(proj3, proj3, hist, w_dw, row(b_dw), row(g_ln), row(b_ln))


SB_TQ = 256
SB_TK = 128


def _sb_prompt_kernel(q_ref, k_ref, v_ref, b_ref, uo_ref, o_ref, kb_ref, vb_ref):
    qi = pl.program_id(2)

    @pl.when(qi == 0)
    def _():
        kb_ref[...] = k_ref[0].astype(jnp.bfloat16)
        vb_ref[...] = v_ref[0].astype(jnp.bfloat16)

    q = q_ref[0].astype(jnp.bfloat16)
    bias = b_ref[0]
    uo = uo_ref[...]
    delta = (lax.broadcasted_iota(jnp.int32, (SB_TQ, SB_TK), 1)
             - lax.broadcasted_iota(jnp.int32, (SB_TQ, SB_TK), 0))
    n_kb = (qi + 1) * (SB_TQ // SB_TK)

    def body(i, carry):
        acc, run = carry
        kb = n_kb - 1 - i
        ks = pl.ds(pl.multiple_of(kb * SB_TK, SB_TK), SB_TK)
        s = lax.dot_general(q, kb_ref[ks, :], _NT, preferred_element_type=jnp.float32)
        z = s * SCALE + bias
        mask = delta < (qi * SB_TQ - kb * SB_TK)
        log_keep = jnp.where(mask, -(jnp.maximum(z, 0.0) + _softplus_neg_abs(z)), 0.0)
        hi, lo = _split_bf16(log_keep)
        at = (jnp.dot(hi, uo, preferred_element_type=jnp.float32)
              + jnp.dot(lo, uo, preferred_element_type=jnp.float32))
        after = at[:, :SB_TK] + run
        a = jnp.exp(jnp.where(mask, log_keep + z + after, MASKED))
        acc = acc + jnp.dot(a.astype(jnp.bfloat16), vb_ref[ks, :], preferred_element_type=jnp.float32)
        return acc, run + at[:, SB_TK:]

    zero = jnp.zeros((SB_TQ, HEAD_DIM), jnp.float32)
    acc, _ = lax.fori_loop(0, n_kb, body, (zero, zero))
    o_ref[0] = acc.astype(jnp.bfloat16)


def _sb_prompt(proj3, b_rows, uo, *, q_col, k_col, v_col):
    b, s, _ = proj3.shape
    return pl.pallas_call(
        _sb_prompt_kernel,
        grid=(b, SB_HEADS, s // SB_TQ),
        in_specs=[
            pl.BlockSpec((1, SB_TQ, HEAD_DIM), lambda i, h, t: (i, t, q_col + h)),
            pl.BlockSpec((1, s, HEAD_DIM), lambda i, h, t: (i, 0, k_col + h)),
            pl.BlockSpec((1, s, HEAD_DIM), lambda i, h, t: (i, 0, v_col + h)),
            pl.BlockSpec((1, 1, LANES), lambda i, h, t: (h, 0, 0)),
            pl.BlockSpec((SB_TK, 2 * SB_TK), lambda i, h, t: (0, 0)),
        ],
        out_specs=pl.BlockSpec((1, SB_TQ, HEAD_DIM), lambda i, h, t: (i, t, h)),
        out_shape=jax.ShapeDtypeStruct((b, s, SB_HEADS * HEAD_DIM), jnp.bfloat16),
        scratch_shapes=[pltpu.VMEM((s, HEAD_DIM), jnp.bfloat16), pltpu.VMEM((s, HEAD_DIM), jnp.bfloat16)],
        compiler_params=_params("arbitrary", "arbitrary", "arbitrary"),
        name="sb_prompt",
    )(proj3, proj3, proj3, b_rows, uo)


MEM_TQ = 512


def _mem_prompt_kernel(q_ref, k_ref, v_ref, o_ref):
    q = q_ref[0].astype(jnp.bfloat16)
    k = k_ref[0].astype(jnp.bfloat16)
    s = lax.dot_general(q, k, _NT, preferred_element_type=jnp.float32) * SCALE
    e = jnp.exp(s - jnp.max(s, axis=-1, keepdims=True))
    p = e / jnp.sum(e, axis=-1, keepdims=True)
    o = jnp.dot(p.astype(jnp.bfloat16), v_ref[0].astype(jnp.bfloat16), preferred_element_type=jnp.float32)
    o_ref[0] = o.astype(jnp.bfloat16)


def _mem_prompt(proj3, mkv3, *, q_col):
    b, s, _ = proj3.shape
    return pl.pallas_call(
        _mem_prompt_kernel,
        grid=(b, MEM_HEADS, s // MEM_TQ),
        in_specs=[
            pl.BlockSpec((1, MEM_TQ, HEAD_DIM), lambda i, h, t: (i, t, q_col + h)),
            pl.BlockSpec((1, MEM_LEN, HEAD_DIM), lambda i, h, t: (i, 0, h)),
            pl.BlockSpec((1, MEM_LEN, HEAD_DIM), lambda i, h, t: (i, 0, MEM_HEADS + h)),
        ],
        out_specs=pl.BlockSpec((1, MEM_TQ, HEAD_DIM), lambda i, h, t: (i, t, h)),
        out_shape=jax.ShapeDtypeStruct((b, s, MEM_HEADS * HEAD_DIM), jnp.bfloat16),
        compiler_params=_params("arbitrary", "arbitrary", "arbitrary"),
        name="mem_prompt",
    )(proj3, mkv3, mkv3)


def _sample_attn_kernel(pt_ref, q_ref, kn_ref, vn_ref, brow_ref, uo_ref, qm_ref, mk_ref, mv_ref, *rest, n_pages):
    k_pages = rest[:n_pages]
    v_pages = rest[n_pages:2 * n_pages]
    o_ref, om_ref = rest[2 * n_pages], rest[2 * n_pages + 1]
    del pt_ref
    n_q = q_ref.shape[1]
    q = q_ref[0].astype(jnp.bfloat16)
    brow = brow_ref[...]
    uo = uo_ref[...]
    row = lax.broadcasted_iota(jnp.int32, (n_q, LANES), 0)
    lane = lax.broadcasted_iota(jnp.int32, (n_q, LANES), 1)
    head_bits = SB_HEADS.bit_length() - 1
    same_head = (row & (SB_HEADS - 1)) == (lane & (SB_HEADS - 1))

    def block(s_blk, valid, run):
        z = s_blk * SCALE + brow
        log_keep = jnp.where(valid, -(jnp.maximum(z, 0.0) + _softplus_neg_abs(z)), 0.0)
        return z, log_keep

    def cumsum(log_keep):
        hi, lo = _split_bf16(log_keep)
        return (jnp.dot(hi, uo, preferred_element_type=jnp.float32)
                + jnp.dot(lo, uo, preferred_element_type=jnp.float32))

    valid_new = jnp.logical_and(same_head, (lane >> head_bits) < (row >> head_bits))
    s_new = lax.dot_general(q, kn_ref[0].astype(jnp.bfloat16), _NT, preferred_element_type=jnp.float32)
    z, log_keep = block(s_new, valid_new, None)
    at = cumsum(log_keep)
    a = jnp.exp(jnp.where(valid_new, log_keep + z + at[:, :LANES], MASKED))
    acc = jnp.dot(a.astype(jnp.bfloat16), vn_ref[0].astype(jnp.bfloat16), preferred_element_type=jnp.float32)
    run = at[:, LANES:]

    n_blk = PAGE_ROWS // LANES
    for p in range(n_pages - 1, -1, -1):
        kb = k_pages[p][...].astype(jnp.bfloat16)
        s_page = lax.dot_general(q, kb, _NT, preferred_element_type=jnp.float32)
        zs, lks = [], []
        for c in range(n_blk):
            z, log_keep = block(s_page[:, c * LANES:(c + 1) * LANES], same_head, None)
            zs.append(z)
            lks.append(log_keep)
        at = cumsum(jnp.concatenate(lks, axis=0))
        pieces = [None] * n_blk
        for c in range(n_blk - 1, -1, -1):
            at_c = at[c * n_q:(c + 1) * n_q, :]
            arg = jnp.where(same_head, lks[c] + zs[c] + at_c[:, :LANES] + run, MASKED)
            pieces[c] = jnp.exp(arg).astype(jnp.bfloat16)
            run = run + at_c[:, LANES:]
        a_page = jnp.concatenate(pieces, axis=1)
        acc = acc + jnp.dot(a_page, v_pages[p][...].astype(jnp.bfloat16), preferred_element_type=jnp.float32)
    o_ref[0] = acc.astype(jnp.bfloat16)

    n_qm = qm_ref.shape[1]
    n_ml = mk_ref.shape[1]
    qm = qm_ref[0].astype(jnp.bfloat16)
    sm = lax.dot_general(qm, mk_ref[0].astype(jnp.bfloat16), _NT, preferred_element_type=jnp.float32) * SCALE
    mrow = lax.broadcasted_iota(jnp.int32, (n_qm, n_ml), 0)
    mlane = lax.broadcasted_iota(jnp.int32, (n_qm, n_ml), 1)
    sm = jnp.where((mrow & (MEM_HEADS - 1)) == (mlane & (MEM_HEADS - 1)), sm, MASKED)
    e = jnp.exp(sm - jnp.max(sm, axis=-1, keepdims=True))
    pm = e / jnp.sum(e, axis=-1, keepdims=True)
    om = jnp.dot(pm.astype(jnp.bfloat16), mv_ref[0].astype(jnp.bfloat16), preferred_element_type=jnp.float32)
    om_ref[0] = om.astype(jnp.bfloat16)


def _sample_attn(page_table, q, k_new, v_new, brow, uo, qm, mem_k, mem_v, cache_k2d, cache_v2d):
    bs, n_pages = page_table.shape
    n_q = q.shape[1]
    n_qm = qm.shape[1]
    n_ml = mem_k.shape[1]
    kern = functools.partial(_sample_attn_kernel, n_pages=n_pages)

    def page_spec(p):
        return pl.BlockSpec((PAGE_ROWS, HEAD_DIM), lambda i, pt: (pt[i, p], 0))

    grid_spec = pltpu.PrefetchScalarGridSpec(
        num_scalar_prefetch=1,
        grid=(bs,),
        in_specs=[
            pl.BlockSpec((1, n_q, HEAD_DIM), lambda i, pt: (i, 0, 0)),
            pl.BlockSpec((1, LANES, HEAD_DIM), lambda i, pt: (i, 0, 0)),
            pl.BlockSpec((1, LANES, HEAD_DIM), lambda i, pt: (i, 0, 0)),
            pl.BlockSpec((n_q, LANES), lambda i, pt: (0, 0)),
            pl.BlockSpec((LANES, 2 * LANES), lambda i, pt: (0, 0)),
            pl.BlockSpec((1, n_qm, HEAD_DIM), lambda i, pt: (i, 0, 0)),
            pl.BlockSpec((1, n_ml, HEAD_DIM), lambda i, pt: (i, 0, 0)),
            pl.BlockSpec((1, n_ml, HEAD_DIM), lambda i, pt: (i, 0, 0)),
        ] + [page_spec(p) for p in range(n_pages)] + [page_spec(p) for p in range(n_pages)],
        out_specs=[
            pl.BlockSpec((1, n_q, HEAD_DIM), lambda i, pt: (i, 0, 0)),
            pl.BlockSpec((1, n_qm, HEAD_DIM), lambda i, pt: (i, 0, 0)),
        ],
    )
    return pl.pallas_call(
        kern,
        grid_spec=grid_spec,
        out_shape=[
            jax.ShapeDtypeStruct((bs, n_q, HEAD_DIM), jnp.bfloat16),
            jax.ShapeDtypeStruct((bs, n_qm, HEAD_DIM), jnp.bfloat16),
        ],
        compiler_params=_params("arbitrary"),
        name="sample_attn",
    )(page_table, q, k_new, v_new, brow, uo, qm, mem_k, mem_v,
      *([cache_k2d] * n_pages), *([cache_v2d] * n_pages))


def _merge_kernel(x_ref, c_ref, osb_ref, om_ref, ga_ref, gb_ref, gm_ref, wc_ref, wsb_ref, wm_ref, wo_ref, o_ref):
    n = pl.program_id(1)

    @pl.when(n == 0)
    def _():
        o_ref[...] = x_ref[...]

    y_conv = jnp.dot(c_ref[...], wc_ref[...], preferred_element_type=jnp.float32)
    y_sb = jnp.dot(osb_ref[...], wsb_ref[...], preferred_element_type=jnp.float32)
    y_mem = jnp.dot(om_ref[...], wm_ref[...], preferred_element_type=jnp.float32)
    merged = ga_ref[...] * y_conv + gb_ref[...] * y_sb + gm_ref[...] * y_mem
    o_ref[...] += jnp.dot(merged.astype(jnp.bfloat16), wo_ref[...], preferred_element_type=jnp.float32)


def _merge(x, c, osb, om, proj, wc, wsb, wm, wo, *, tm, gate_col):
    m, d = x.shape
    tn = COL_BLOCK
    nb = d // tn
    return pl.pallas_call(
        _merge_kernel,
        grid=(m // tm, nb),
        in_specs=[
            pl.BlockSpec((tm, d), lambda i, n: (i, 0)),
            pl.BlockSpec((tm, c.shape[1]), lambda i, n: (i, 0)),
            pl.BlockSpec((tm, osb.shape[1]), lambda i, n: (i, 0)),
            pl.BlockSpec((tm, om.shape[1]), lambda i, n: (i, 0)),
            pl.BlockSpec((tm, tn), lambda i, n: (i, gate_col + n)),
            pl.BlockSpec((tm, tn), lambda i, n: (i, gate_col + nb + n)),
            pl.BlockSpec((tm, tn), lambda i, n: (i, gate_col + 2 * nb + n)),
            pl.BlockSpec((wc.shape[0], tn), lambda i, n: (0, n)),
            pl.BlockSpec((wsb.shape[0], tn), lambda i, n: (0, n)),
            pl.BlockSpec((wm.shape[0], tn), lambda i, n: (0, n)),
            pl.BlockSpec((tn, d), lambda i, n: (n, 0)),
        ],
        out_specs=pl.BlockSpec((tm, d), lambda i, n: (i, 0)),
        out_shape=jax.ShapeDtypeStruct((m, d), jnp.float32),
        compiler_params=_params("arbitrary", "arbitrary"),
        name="merge_out",
    )(x, c, osb, om, proj, proj, proj, wc, wsb, wm, wo)


def _mlp_kernel(x_ref, g_ref, wup_ref, wdn_ref, o_ref, h_ref, *, tm):
    f = pl.program_id(1)

    @pl.when(f == 0)
    def _():
        _rms_rows_to_bf16(x_ref, g_ref, h_ref, tm)
        o_ref[...] = x_ref[...]

    a = jnp.dot(h_ref[...], wup_ref[...], preferred_element_type=jnp.float32)
    a = jnp.square(jnp.maximum(a, 0.0)).astype(jnp.bfloat16)
    o_ref[...] += jnp.dot(a, wdn_ref[...], preferred_element_type=jnp.float32)


def _mlp(x, g, wup, wdn, *, tm, tf):
    m, d = x.shape
    ff = wup.shape[1]
    kern = functools.partial(_mlp_kernel, tm=tm)
    return pl.pallas_call(
        kern,
        grid=(m // tm, ff // tf),
        in_specs=[
            pl.BlockSpec((tm, d), lambda i, f: (i, 0)),
            pl.BlockSpec((1, d), lambda i, f: (0, 0)),
            pl.BlockSpec((d, tf), lambda i, f: (0, f)),
            pl.BlockSpec((tf, d), lambda i, f: (f, 0)),
        ],
        out_specs=pl.BlockSpec((tm, d), lambda i, f: (i, 0)),
        out_shape=jax.ShapeDtypeStruct((m, d), jnp.float32),
        scratch_shapes=[pltpu.VMEM((tm, d), jnp.bfloat16)],
        compiler_params=_params("arbitrary", "arbitrary"),
        name="mlp",
    )(x, g.reshape(1, d), wup, wdn)


def _cumsum_matrix(n_groups):
    src = jnp.arange(LANES)[:, None]
    dst = jnp.arange(LANES)[None, :]
    same = (src % n_groups) == (dst % n_groups)
    later = (src // n_groups) > (dst // n_groups)
    return jnp.concatenate([jnp.logical_and(same, later), same], axis=1).astype(jnp.bfloat16)


def kernel(x_prompt, x_sample, mem_prompt, cache_sb_k, cache_sb_v, page_table, state_conv, cache_mem_k, cache_mem_v, g_mix, w_in, b_gate, w_dw, b_dw, g_conv_ln, b_conv_ln, w_conv_out, g_q_sb, g_k_sb, b_sb, w_sb_out, g_mem, w_mem_kv, g_q_mem, g_k_mem, w_mem_out, w_o, g_mlp, w_up, w_down):
    depth = w_in.shape[0]
    assert depth == 1, "single-layer step"
    bp, sp, d_model = x_prompt.shape
    bs, ss, _ = x_sample.shape
    cw = w_dw.shape[2]
    sb_w = SB_HEADS * HEAD_DIM
    mem_w = MEM_HEADS * HEAD_DIM
    in_cols = w_in.shape[2]
    assert in_cols == 2 * cw + 3 * sb_w + mem_w + 3 * d_model
    bf = lambda w: w.astype(jnp.bfloat16)
    l = 0

    q_sb0, k_sb0, v_sb0, q_mem0, gate0 = 2 * cw, 2 * cw + sb_w, 2 * cw + 2 * sb_w, 2 * cw + 3 * sb_w, 2 * cw + 3 * sb_w + mem_w
    cb = lambda c: c // COL_BLOCK
    norm_blocks = list(range(cb(q_sb0), cb(v_sb0))) + list(range(cb(q_mem0), cb(gate0)))
    ones = lambda n: jnp.ones((n,), jnp.float32)
    gcol = jnp.concatenate([ones(2 * cw), jnp.tile(g_q_sb[l], SB_HEADS), jnp.tile(g_k_sb[l], SB_HEADS), ones(sb_w),
                            jnp.tile(g_q_mem[l], MEM_HEADS), ones(3 * d_model)]).reshape(1, in_cols)
    bcol = jnp.concatenate([jnp.zeros((gate0,), jnp.float32), b_gate[l]]).reshape(1, in_cols)
    w_in_b = bf(w_in[l])

    mem2 = mem_prompt.reshape(bp * MEM_LEN, d_model)
    gcol_m = jnp.concatenate([jnp.tile(g_k_mem[l], MEM_HEADS), ones(mem_w)]).reshape(1, 2 * mem_w)
    mkv = _norm_proj(mem2, g_mem[l], bf(w_mem_kv[l]), gcol_m, jnp.zeros((1, 2 * mem_w), jnp.float32),
                     tm=256, tn=mem_w, norm_blocks=[0], gate_start=2)

    proj_p = _norm_proj(x_prompt.reshape(bp * sp, d_model), g_mix[l], w_in_b, gcol, bcol,
                        tm=1024, tn=COL_BLOCK, norm_blocks=norm_blocks, gate_start=cb(gate0))
    proj_s = _norm_proj(x_sample.reshape(bs * ss, d_model), g_mix[l], w_in_b, gcol, bcol,
                        tm=bs * ss, tn=COL_BLOCK, norm_blocks=norm_blocks, gate_start=cb(gate0))
    proj_p3 = proj_p.reshape(bp, sp, in_cols)
    proj_s3 = proj_s.reshape(bs, ss, in_cols)

    c_p, conv_p = _conv_branch(proj_p3, jnp.zeros((bp, CONV_HIST, cw), jnp.float32), w_dw[l], b_dw[l],
                               g_conv_ln[l], b_conv_ln[l], tm=256)
    c_s, conv_s = _conv_branch(proj_s3, state_conv[l], w_dw[l], b_dw[l], g_conv_ln[l], b_conv_ln[l], tm=ss)

    hb = lambda c: c // HEAD_DIM
    b_rows = jnp.broadcast_to(b_sb[l][:, None, None], (SB_HEADS, 1, LANES)).astype(jnp.float32)
    o_sb_p = _sb_prompt(proj_p3, b_rows, _cumsum_matrix(1), q_col=hb(q_sb0), k_col=hb(k_sb0), v_col=hb(v_sb0))
    o_mem_p = _mem_prompt(proj_p3, mkv.reshape(bp, MEM_LEN, 2 * mem_w), q_col=hb(q_mem0))

    n_q = ss * SB_HEADS
    q_s = proj_s[:, q_sb0:k_sb0].reshape(bs, n_q, HEAD_DIM)
    pad_rows = lambda a: jnp.pad(a, ((0, 0), (0, LANES - a.shape[1]), (0, 0)))
    k_new = pad_rows(proj_s[:, k_sb0:v_sb0].reshape(bs, n_q, HEAD_DIM))
    v_new = pad_rows(proj_s[:, v_sb0:q_mem0].reshape(bs, n_q, HEAD_DIM))
    qm_s = proj_s[:, q_mem0:gate0].reshape(bs, ss * MEM_HEADS, HEAD_DIM)
    brow = jnp.broadcast_to(jnp.tile(b_sb[l], ss)[:, None], (n_q, LANES)).astype(jnp.float32)
    n_pool = cache_sb_k.shape[1]
    o_sb_s, o_mem_s = _sample_attn(
        page_table, q_s, k_new, v_new, brow, _cumsum_matrix(SB_HEADS), qm_s,
        cache_mem_k[l].reshape(bs, MEM_LEN * MEM_HEADS, HEAD_DIM),
        cache_mem_v[l].reshape(bs, MEM_LEN * MEM_HEADS, HEAD_DIM),
        cache_sb_k[l].reshape(n_pool * PAGE_ROWS, HEAD_DIM),
        cache_sb_v[l].reshape(n_pool * PAGE_ROWS, HEAD_DIM))

    wc, wsb, wm, wo = bf(w_conv_out[l]), bf(w_sb_out[l]), bf(w_mem_out[l]), bf(w_o[l])
    x1_p = _merge(x_prompt.reshape(bp * sp, d_model), c_p.reshape(bp * sp, cw), o_sb_p.reshape(bp * sp, sb_w),
                  o_mem_p.reshape(bp * sp, mem_w), proj_p, wc, wsb, wm, wo, tm=512, gate_col=cb(gate0))
    x1_s = _merge(x_sample.reshape(bs * ss, d_model), c_s.reshape(bs * ss, cw), o_sb_s.reshape(bs * ss, sb_w),
                  o_mem_s.reshape(bs * ss, mem_w), proj_s, wc, wsb, wm, wo, tm=bs * ss, gate_col=cb(gate0))
    wup, wdn = bf(w_up[l]), bf(w_down[l])
    y_p = _mlp(x1_p, g_mlp[l], wup, wdn, tm=512, tf=512).reshape(bp, sp, d_model)
    y_s = _mlp(x1_s, g_mlp[l], wup, wdn, tm=bs * ss, tf=512).reshape(bs, ss, d_model)

    heads = lambda a, b, s: a.reshape(1, b, s, SB_HEADS, HEAD_DIM)
    return (y_p, y_s,
            heads(proj_p[:, k_sb0:v_sb0], bp, sp), heads(proj_p[:, v_sb0:q_mem0], bp, sp),
            heads(proj_s[:, k_sb0:v_sb0], bs, ss), heads(proj_s[:, v_sb0:q_mem0], bs, ss),
            conv_p[None], conv_s[None],
            mkv[:, :mem_w].reshape(1, bp, MEM_LEN, MEM_HEADS, HEAD_DIM),
            mkv[:, mem_w:].reshape(1, bp, MEM_LEN, MEM_HEADS, HEAD_DIM))
```
